```python
import jax, jax.numpy as jnp
from jax import lax
import numpy as np

D_MODEL = 1024
BATCH = 8
SEQ = 2048
DEPTH = 1
DEC_BATCH = 128
DEC_SEQ = 1
PAST_LEN = 16384
PAGE_SIZE = 128

C_POOL = D_MODEL // 2
POOL_WINDOWS = (2, 4, 8, 16)
N_POOL_GROUPS = len(POOL_WINDOWS)
POOL_GROUP = C_POOL // N_POOL_GROUPS
POOL_BUF = max(POOL_WINDOWS) - 1
C_GATE = D_MODEL - C_POOL
N_GATE_HEADS = 4
GATE_HEAD = C_GATE // N_GATE_HEADS
CHUNK = 128
D_MIX = C_POOL + C_GATE
D_IN = C_POOL + 2 * C_GATE
N_EXPERTS = 256
TOP_K = 8
N_GROUPS = 8
TOPK_GROUPS = 4
D_EXPERT = 256
D_SHARED = 256
ROUTED_SCALE = 2.5
EXPERT_BLOCK = 128
EPS = 1e-6

kernel_name = 'hybrid_pool_gmlp_moe_adaln_step'


def rmsnorm(x, g):
    xf = x.astype(jnp.float32)
    y = xf * lax.rsqrt(jnp.mean(xf * xf, axis=-1, keepdims=True) + EPS)
    return (y * g.astype(jnp.float32)).astype(x.dtype)


def layernorm(x, g, b):
    xf = x.astype(jnp.float32)
    mu = jnp.mean(xf, axis=-1, keepdims=True)
    var = jnp.mean(jnp.square(xf - mu), axis=-1, keepdims=True)
    y = (xf - mu) * lax.rsqrt(var + EPS) * g.astype(jnp.float32) + b.astype(jnp.float32)
    return y.astype(x.dtype)


def swiglu(x, wg, wu, wd):
    return (jax.nn.silu(x @ wg) * (x @ wu)) @ wd


def pool_mixer(a, buf, start_pos, pool_w, pool_scale):
    T = a.shape[1]
    ext = jnp.concatenate([buf.astype(a.dtype), a], axis=1)
    csum = jnp.cumsum(ext.astype(jnp.float32), axis=1)
    csum = jnp.pad(csum, ((0, 0), (1, 0), (0, 0)))
    pos = start_pos + jnp.arange(T)
    outs = []
    for gi, w in enumerate(POOL_WINDOWS):
        sl = slice(gi * POOL_GROUP, (gi + 1) * POOL_GROUP)
        hi = csum[:, POOL_BUF + 1:, sl]
        lo = csum[:, POOL_BUF + 1 - w:POOL_BUF + 1 - w + T, sl]
        cnt = jnp.minimum(pos + 1, w).astype(jnp.float32)[None, :, None]
        d = (hi - lo) / cnt - a[..., sl].astype(jnp.float32)
        outs.append(jnp.einsum('btc,cd->btd', d, pool_w[gi].astype(jnp.float32)))
    out = jnp.concatenate(outs, axis=-1) * pool_scale.astype(jnp.float32)
    return out.astype(a.dtype), ext[:, -POOL_BUF:]


def spatial_gate(u, v, w_s, b_s):
    B, T, H, Dh = v.shape
    n_chunks = -(-T // CHUNK)
    pad = n_chunks * CHUNK - T
    vp = jnp.pad(v, ((0, 0), (0, pad), (0, 0), (0, 0))).reshape(B, n_chunks, CHUNK, H, Dh)
    mask = jnp.tril(jnp.ones((CHUNK, CHUNK), dtype=bool))
    ws = jnp.where(mask[None], w_s, jnp.zeros_like(w_s))
    z = jnp.einsum('hts,bcshd->bcthd', ws, vp) + b_s.T[None, None, :, :, None]
    z = z.reshape(B, n_chunks * CHUNK, H, Dh)[:, :T]
    return u * z


def routed_experts(h, idx, w, we_gate, we_up, we_down):
    T, D = h.shape
    S = T * TOP_K
    n_blocks = -(-S // EXPERT_BLOCK) + N_EXPERTS
    flat_e = idx.reshape(-1)
    flat_tok = jnp.repeat(jnp.arange(T, dtype=jnp.int32), TOP_K)
    flat_w = w.reshape(-1)
    order = jnp.argsort(flat_e)
    e_sorted = flat_e[order]
    tok_sorted = flat_tok[order]
    w_sorted = flat_w[order]
    counts = jnp.bincount(flat_e, length=N_EXPERTS)
    padded = (counts + EXPERT_BLOCK - 1) // EXPERT_BLOCK * EXPERT_BLOCK
    start = jnp.cumsum(counts) - counts
    pad_end = jnp.cumsum(padded)
    pad_start = pad_end - padded
    dest = pad_start[e_sorted] + jnp.arange(S, dtype=jnp.int32) - start[e_sorted]
    buf_tok = jnp.full((n_blocks * EXPERT_BLOCK,), T, jnp.int32).at[dest].set(tok_sorted)
    buf_w = jnp.zeros((n_blocks * EXPERT_BLOCK,), h.dtype).at[dest].set(w_sorted)
    block_start = jnp.arange(n_blocks, dtype=jnp.int32) * EXPERT_BLOCK
    block_expert = jnp.minimum(jnp.searchsorted(pad_end, block_start, side='right'), N_EXPERTS - 1)
    h_pad = jnp.concatenate([h, jnp.zeros((1, D), h.dtype)], axis=0)

    def run_block(args):
        toks, e = args
        return swiglu(h_pad[toks], we_gate[e], we_up[e], we_down[e])

    out = lax.map(run_block, (buf_tok.reshape(n_blocks, EXPERT_BLOCK), block_expert))
    contrib = out.reshape(-1, D) * buf_w[:, None]
    return jax.ops.segment_sum(contrib, buf_tok, num_segments=T + 1)[:T]


def moe_ffn(h, router_w, router_bias, we_gate, we_up, we_down, ws_gate, ws_up, ws_down):
    T = h.shape[0]
    scores = jax.nn.sigmoid(jnp.einsum('td,de->te', h.astype(jnp.float32), router_w.astype(jnp.float32)))
    sel = scores + router_bias.astype(jnp.float32)
    grp_score = lax.top_k(sel.reshape(T, N_GROUPS, N_EXPERTS // N_GROUPS), 2)[0].sum(-1)
    _, top_grp = lax.top_k(grp_score, TOPK_GROUPS)
    grp_mask = jax.nn.one_hot(top_grp, N_GROUPS, dtype=jnp.float32).sum(1)
    expert_mask = jnp.repeat(grp_mask, N_EXPERTS // N_GROUPS, axis=1) > 0
    _, idx = lax.top_k(jnp.where(expert_mask, sel, -jnp.inf), TOP_K)
    wsel = jnp.take_along_axis(scores, idx, axis=1)
    wsel = wsel / jnp.sum(wsel, axis=-1, keepdims=True) * ROUTED_SCALE
    routed = routed_experts(h, idx, wsel.astype(h.dtype), we_gate, we_up, we_down)
    return routed + swiglu(h, ws_gate, ws_up, ws_down)


def layer(x, c, buf, start_pos, g_mix, g_ffn, w_ada, b_ada, w_in, pool_w, pool_scale,
          gate_ln_g, gate_ln_b, w_s, b_s, w_out, router_w, router_bias,
          we_gate, we_up, we_down, ws_gate, ws_up, ws_down):
    B, T, D = x.shape
    mod = jnp.einsum('bd,de->be', jax.nn.silu(c), w_ada) + b_ada
    sh1, sc1, g1, sh2, sc2, g2 = jnp.split(mod, 6, axis=-1)
    h = rmsnorm(x, g_mix) * (1 + sc1[:, None]) + sh1[:, None]
    z = jnp.einsum('btd,de->bte', h, w_in)
    a = z[..., :C_POOL]
    uv = jax.nn.gelu(z[..., C_POOL:], approximate=False)
    u = uv[..., :C_GATE].reshape(B, T, N_GATE_HEADS, GATE_HEAD)
    v = layernorm(uv[..., C_GATE:].reshape(B, T, N_GATE_HEADS, GATE_HEAD), gate_ln_g, gate_ln_b)
    pool_out, new_buf = pool_mixer(a, buf, start_pos, pool_w, pool_scale)
    gate_out = spatial_gate(u, v, w_s, b_s).reshape(B, T, C_GATE)
    mix = jnp.concatenate([pool_out, gate_out], axis=-1)
    x = x + g1[:, None] * jnp.einsum('bte,ed->btd', mix, w_out)
    h2 = rmsnorm(x, g_ffn) * (1 + sc2[:, None]) + sh2[:, None]
    ffn = moe_ffn(h2.reshape(B * T, D), router_w, router_bias, we_gate, we_up, we_down,
                  ws_gate, ws_up, ws_down).reshape(B, T, D)
    x = x + g2[:, None] * ffn
    return x, new_buf, v


def setup_inputs(seed: int = 0) -> dict:
    key = jax.random.key(seed)
    ks = jax.random.split(key, 32)
    f32 = jnp.float32
    nrm = lambda k, shape, s: jax.random.normal(k, shape, f32) * s
    D = D_MODEL
    return {
        'x_prompt': nrm(ks[0], (BATCH, SEQ, D), 1.0),
        'x_sample': nrm(ks[1], (DEC_BATCH, DEC_SEQ, D), 1.0),
        'state_pool': nrm(ks[2], (DEPTH, DEC_BATCH, POOL_BUF, C_POOL), 1.0),
        'c_prompt': nrm(ks[3], (BATCH, D), 1.0),
        'c_sample': nrm(ks[4], (DEC_BATCH, D), 1.0),
        'norm_mix_g': 1.0 + nrm(ks[5], (DEPTH, D), 0.05),
        'norm_ffn_g': 1.0 + nrm(ks[6], (DEPTH, D), 0.05),
        'norm_final_g': 1.0 + nrm(ks[7], (D,), 0.05),
        'w_ada': nrm(ks[8], (DEPTH, D, 6 * D), 0.5 * D ** -0.5),
        'b_ada': nrm(ks[9], (DEPTH, 6 * D), 0.01),
        'w_in': nrm(ks[10], (DEPTH, D, D_IN), D ** -0.5),
        'pool_w': nrm(ks[11], (DEPTH, N_POOL_GROUPS, POOL_GROUP, POOL_GROUP), POOL_GROUP ** -0.5),
        'pool_scale': 1.0 + nrm(ks[12], (DEPTH, C_POOL), 0.1),
        'gate_ln_g': 1.0 + nrm(ks[13], (DEPTH, N_GATE_HEADS, GATE_HEAD), 0.05),
        'gate_ln_b': nrm(ks[14], (DEPTH, N_GATE_HEADS, GATE_HEAD), 0.01),
        'w_s': nrm(ks[15], (DEPTH, N_GATE_HEADS, CHUNK, CHUNK), 0.5 * CHUNK ** -0.5),
        'b_s': 1.0 + nrm(ks[16], (DEPTH, N_GATE_HEADS, CHUNK), 0.01),
        'w_out': nrm(ks[17], (DEPTH, D_MIX, D), D_MIX ** -0.5),
        'router_w': nrm(ks[18], (DEPTH, D, N_EXPERTS), D ** -0.5),
        'router_bias': nrm(ks[19], (DEPTH, N_EXPERTS), 0.01),
        'we_gate': nrm(ks[20], (DEPTH, N_EXPERTS, D, D_EXPERT), D ** -0.5),
        'we_up': nrm(ks[21], (DEPTH, N_EXPERTS, D, D_EXPERT), D ** -0.5),
        'we_down': nrm(ks[22], (DEPTH, N_EXPERTS, D_EXPERT, D), D_EXPERT ** -0.5),
        'ws_gate': nrm(ks[23], (DEPTH, D, D_SHARED), D ** -0.5),
        'ws_up': nrm(ks[24], (DEPTH, D, D_SHARED), D ** -0.5),
        'ws_down': nrm(ks[25], (DEPTH, D_SHARED, D), D_SHARED ** -0.5),
    }


def reference(x_prompt, x_sample, state_pool, c_prompt, c_sample, norm_mix_g, norm_ffn_g,
              norm_final_g, w_ada, b_ada, w_in, pool_w, pool_scale, gate_ln_g, gate_ln_b,
              w_s, b_s, w_out, router_w, router_bias, we_gate, we_up, we_down,
              ws_gate, ws_up, ws_down):
    xp = x_prompt
    xs = x_sample
    pool_p, pool_s, v_s = [], [], []
    for l in range(DEPTH):
        params = (norm_mix_g[l], norm_ffn_g[l], w_ada[l], b_ada[l], w_in[l], pool_w[l],
                  pool_scale[l], gate_ln_g[l], gate_ln_b[l], w_s[l], b_s[l], w_out[l],
                  router_w[l], router_bias[l], we_gate[l], we_up[l], we_down[l],
                  ws_gate[l], ws_up[l], ws_down[l])
        zero_buf = jnp.zeros((BATCH, POOL_BUF, C_POOL), xp.dtype)
        xp, buf_p, _ = layer(xp, c_prompt, zero_buf, 0, *params)
        xs, buf_s, vs = layer(xs, c_sample, state_pool[l], PAST_LEN, *params)
        pool_p.append(buf_p)
        pool_s.append(buf_s)
        v_s.append(vs)
    y_prompt = rmsnorm(xp, norm_final_g)
    y_sample = rmsnorm(xs, norm_final_g)
    new_pool_prompt = jnp.stack(pool_p, axis=0)
    new_pool_sample = jnp.stack(pool_s, axis=0)
    new_gate_v_sample = jnp.stack(v_s, axis=0)
    return (y_prompt, y_sample, new_pool_prompt, new_pool_sample, new_gate_v_sample)
```

```python
import functools
import math

import jax
import jax.numpy as jnp
from jax import lax
from jax.experimental import pallas as pl
from jax.experimental.pallas import tpu as pltpu

F32 = jnp.float32
BF16 = jnp.bfloat16
I32 = jnp.int32
U32 = jnp.uint32

EPS = 1e-6
POOL_WINDOWS = (2, 4, 8, 16)
POOL_BUF = max(POOL_WINDOWS) - 1
CHUNK = 128
TOP_K = 8
N_GROUPS = 8
TOPK_GROUPS = 4
ROUTED_SCALE = 2.5
PAST_LEN = 16384

LANES = 128
ROW_BLOCK = 128
TOKEN_BLOCK = 128
MIX_TILE = 256
POOL_ROW0 = 24
VMEM_LIMIT = 48 * 1024 * 1024

NEG_INF = float("-inf")
_NT = (((1,), (1,)), ((), ()))


def _dot(a, b):
    return jnp.dot(a, b, preferred_element_type=F32)


def _rms(x, g):
    return x * lax.rsqrt(jnp.mean(x * x, axis=-1, keepdims=True) + EPS) * g


def _gelu(x):
    return 0.5 * x * (1.0 + lax.erf(x * math.sqrt(0.5)))


def _silu(x):
    return x * jax.nn.sigmoid(x)


def _pack_pair(lo, hi):
    lo_b = lax.bitcast_convert_type(lo.astype(BF16).astype(F32), U32)
    hi_b = lax.bitcast_convert_type(hi.astype(BF16).astype(F32), U32)
    return hi_b | (lo_b >> 16)


def _unpack_pair(w):
    lo = lax.bitcast_convert_type(w << 16, F32)
    hi = lax.bitcast_convert_type(w & jnp.uint32(0xFFFF0000), F32)
    return lo, hi


def _pack_rows(y):
    half = y.shape[1] // 2
    return _pack_pair(y[:, :half], y[:, half:])


def _load_packed_rows(ref):
    nj = ref.shape[1]
    los, his = [], []
    for j in range(nj):
        lo, hi = _unpack_pair(ref[:, j, :])
        los.append(lo.astype(BF16))
        his.append(hi.astype(BF16))
    return jnp.concatenate(los + his, axis=-1)


def _store_packed_rows(ref, words):
    for j in range(ref.shape[1]):
        ref[:, j, :] = words[:, j * LANES:(j + 1) * LANES]


def _mod_kernel(c_ref, w_ref, b_ref, o_ref):
    s = _silu(c_ref[...])
    o_ref[0] = jnp.dot(s, w_ref[...], precision=lax.Precision.HIGHEST,
                       preferred_element_type=F32) + b_ref[0]


def _modulation(c_all, w_ada, b_ada):
    nb, d = c_all.shape
    return pl.pallas_call(
        _mod_kernel,
        grid=(6,),
        in_specs=[pl.BlockSpec((nb, d), lambda j: (0, 0)),
                  pl.BlockSpec((d, d), lambda j: (0, j)),
                  pl.BlockSpec((1, 1, d), lambda j: (j, 0, 0))],
        out_specs=pl.BlockSpec((1, nb, d), lambda j: (j, 0, 0)),
        out_shape=jax.ShapeDtypeStruct((6, nb, d), F32),
        compiler_params=pltpu.CompilerParams(dimension_semantics=("arbitrary",),
                                             vmem_limit_bytes=VMEM_LIMIT),
        name="mod",
    )(c_all, w_ada, b_ada.reshape(6, 1, d))


def _route_topk(sel, scores):
    n_e, n_l = sel.shape
    per = n_e // N_GROUPS
    eid = lax.broadcasted_iota(I32, (n_e, n_l), 0)
    gscore = []
    for g in range(N_GROUPS):
        blk = sel[g * per:(g + 1) * per]
        beid = eid[g * per:(g + 1) * per]
        m1 = jnp.max(blk, axis=0, keepdims=True)
        a1 = jnp.min(jnp.where(blk == m1, beid, n_e), axis=0, keepdims=True)
        m2 = jnp.max(jnp.where(beid == a1, NEG_INF, blk), axis=0, keepdims=True)
        gscore.append(m1 + m2)
    cur = jnp.concatenate(gscore, axis=0)
    gid = lax.broadcasted_iota(I32, (N_GROUPS, n_l), 0)
    chosen = jnp.zeros((N_GROUPS, n_l), F32)
    for _ in range(TOPK_GROUPS):
        m = jnp.max(cur, axis=0, keepdims=True)
        a = jnp.min(jnp.where(cur == m, gid, N_GROUPS), axis=0, keepdims=True)
        pick = gid == a
        chosen = jnp.where(pick, 1.0, chosen)
        cur = jnp.where(pick, NEG_INF, cur)
    parts = []
    for g in range(N_GROUPS):
        on = chosen[g:g + 1] > 0.0
        parts.append(jnp.where(on, sel[g * per:(g + 1) * per], NEG_INF))
    cur = jnp.concatenate(parts, axis=0)
    idx, wts = [], []
    for _ in range(TOP_K):
        m = jnp.max(cur, axis=0, keepdims=True)
        a = jnp.min(jnp.where(cur == m, eid, n_e), axis=0, keepdims=True)
        pick = eid == a
        wts.append(jnp.sum(jnp.where(pick, scores, 0.0), axis=0, keepdims=True))
        idx.append(a)
        cur = jnp.where(pick, NEG_INF, cur)
    total = wts[0]
    for w in wts[1:]:
        total = total + w
    wts = [w / total * ROUTED_SCALE for w in wts]
    return jnp.concatenate(idx, axis=0), jnp.concatenate(wts, axis=0)


def _in_proj(x, gmix, sc1, sh1, w_in_ref, ln_g, ln_b):
    c_pool = 2 * x.shape[1] - w_in_ref.shape[1]
    h = _rms(x, gmix) * (1.0 + sc1) + sh1
    z = _dot(h.astype(BF16), w_in_ref[...])
    c_gate = x.shape[1] - c_pool
    a = z[:, :c_pool]
    uv = _gelu(z[:, c_pool:])
    u = uv[:, :c_gate]
    vr = uv[:, c_gate:]
    vs = []
    for hd in range(c_gate // LANES):
        blk = vr[:, hd * LANES:(hd + 1) * LANES]
        mu = jnp.mean(blk, axis=-1, keepdims=True)
        var = jnp.mean(jnp.square(blk - mu), axis=-1, keepdims=True)
        vs.append((blk - mu) * lax.rsqrt(var + EPS))
    v = jnp.concatenate(vs, axis=-1) * ln_g + ln_b
    return a, u, v


def _pool_project(d_groups, pool_w_ref, pool_scale):
    outs = [_dot(d.astype(BF16), pool_w_ref[gi]) for gi, d in enumerate(d_groups)]
    return jnp.concatenate(outs, axis=-1) * pool_scale


def _post_mix(x, pool_out, gate, g1, w_out_ref, gffn, sc2, sh2, rw_hi_ref, rw_lo_ref, rbias):
    c_pool = pool_out.shape[1]
    mixed = (_dot(pool_out.astype(BF16), w_out_ref[:c_pool, :])
             + _dot(gate.astype(BF16), w_out_ref[c_pool:, :]))
    x1 = x + g1 * mixed
    h2 = _rms(x1, gffn) * (1.0 + sc2) + sh2
    h_hi = h2.astype(BF16)
    h_lo = (h2 - h_hi.astype(F32)).astype(BF16)
    rw_hi = rw_hi_ref[...]
    logits = (lax.dot_general(rw_hi, h_hi, _NT, preferred_element_type=F32)
              + lax.dot_general(rw_lo_ref[...], h_hi, _NT, preferred_element_type=F32)
              + lax.dot_general(rw_hi, h_lo, _NT, preferred_element_type=F32))
    scores = jax.nn.sigmoid(logits)
    idxs, wts = [], []
    for c in range(x.shape[0] // LANES):
        sl = slice(c * LANES, (c + 1) * LANES)
        i_c, w_c = _route_topk(scores[:, sl] + rbias, scores[:, sl])
        idxs.append(i_c)
        wts.append(w_c)
    return x1, _pack_rows(h2), jnp.concatenate(idxs, axis=1), jnp.concatenate(wts, axis=1)


def _mix_prompt_kernel(x_ref, mod_ref, gmix_ref, gffn_ref, w_in_ref, pool_w_ref, pool_scale_ref,
                       ln_g_ref, ln_b_ref, ws_ref, bsb_ref, w_out_ref, rw_hi_ref, rw_lo_ref,
                       rbias_ref,
                       x1_ref, h2_ref, idx_ref, wt_ref, pool_ref,
                       e1, e2, e4, e8):
    t = pl.program_id(1)
    tt = x_ref.shape[1]
    r0 = POOL_ROW0
    x = x_ref[0]
    sh1, sc1, g1 = mod_ref[0, 0], mod_ref[1, 0], mod_ref[2, 0]
    sh2, sc2 = mod_ref[3, 0], mod_ref[4, 0]
    a, u, v = _in_proj(x, gmix_ref[...], sc1, sh1, w_in_ref, ln_g_ref[...], ln_b_ref[...])

    @pl.when(t == 0)
    def _():
        e1[0:r0, :] = jnp.zeros((r0, e1.shape[1]), F32)

    e1[r0:r0 + tt, :] = a
    n = r0 + tt
    e2[0:8, :] = jnp.zeros((8, e2.shape[1]), F32)
    e4[0:8, :] = jnp.zeros((8, e4.shape[1]), F32)
    e2[8:n, :] = e1[8:n, :] + e1[7:n - 1, :]
    e4[8:n, :] = e2[8:n, LANES:] + e2[6:n - 2, LANES:]
    e8[8:n, :] = e4[8:n, LANES:] + e4[4:n - 4, LANES:]
    s16 = e8[r0:n, LANES:] + e8[r0 - 8:n - 8, LANES:]
    sums = (e2[r0:n, 0:LANES], e4[r0:n, 0:LANES], e8[r0:n, 0:LANES], s16)
    pos = lax.broadcasted_iota(I32, (tt, LANES), 0) + t * tt
    d_groups = []
    for gi, w in enumerate(POOL_WINDOWS):
        cnt = jnp.minimum(pos + 1, w).astype(F32)
        d_groups.append(sums[gi] / cnt - a[:, gi * LANES:(gi + 1) * LANES])
    pool_out = _pool_project(d_groups, pool_w_ref, pool_scale_ref[...])
    e1[8:r0, :] = e1[tt + 8:tt + r0, :]

    @pl.when(t == pl.num_programs(1) - 1)
    def _():
        pool_ref[0] = a[tt - POOL_BUF:, :]

    n_heads = ws_ref.shape[0]
    rows = []
    for c in range(tt // CHUNK):
        heads = []
        for hd in range(n_heads):
            vb = v[c * CHUNK:(c + 1) * CHUNK, hd * LANES:(hd + 1) * LANES].astype(BF16)
            heads.append(_dot(ws_ref[hd], vb) + bsb_ref[hd])
        rows.append(jnp.concatenate(heads, axis=-1))
    gate = u * jnp.concatenate(rows, axis=0)

    x1, h2w, idx, wt = _post_mix(x, pool_out, gate, g1, w_out_ref, gffn_ref[...], sc2, sh2,
                                 rw_hi_ref, rw_lo_ref, rbias_ref[...])
    x1_ref[...] = x1
    _store_packed_rows(h2_ref, h2w)
    idx_ref[...] = idx
    wt_ref[...] = wt


def _full(shape):
    nd = len(shape)
    return pl.BlockSpec(shape, lambda *_: (0,) * nd)


def _mix_prompt(x, mod4, weights):
    (gmix, gffn, w_in, pool_w, pool_scale, ln_g, ln_b, ws, bsb, w_out, rw_hi, rw_lo, rbias) = weights
    b, t, d = x.shape
    n_tok = b * t
    tt = MIX_TILE
    nt = t // tt
    c_pool = pool_scale.shape[1]
    nj = d // (2 * LANES)
    in_specs = [pl.BlockSpec((1, tt, d), lambda i, j: (i, j, 0)),
                pl.BlockSpec((6, 1, 1, d), lambda i, j: (0, i, 0, 0))]
    in_specs += [_full(w.shape) for w in weights]
    out_shape = (jax.ShapeDtypeStruct((n_tok, d), F32),
                 jax.ShapeDtypeStruct((n_tok, nj, LANES), U32),
                 jax.ShapeDtypeStruct((TOP_K, n_tok), I32),
                 jax.ShapeDtypeStruct((TOP_K, n_tok), F32),
                 jax.ShapeDtypeStruct((b, POOL_BUF, c_pool), F32))
    out_specs = (pl.BlockSpec((tt, d), lambda i, j: (i * nt + j, 0)),
                 pl.BlockSpec((tt, nj, LANES), lambda i, j: (i * nt + j, 0, 0)),
                 pl.BlockSpec((TOP_K, tt), lambda i, j: (0, i * nt + j)),
                 pl.BlockSpec((TOP_K, tt), lambda i, j: (0, i * nt + j)),
                 pl.BlockSpec((1, POOL_BUF, c_pool), lambda i, j: (i, 0, 0)))
    rows = tt + POOL_ROW0
    scratch = [pltpu.VMEM((rows, c_pool), F32), pltpu.VMEM((rows, c_pool), F32),
               pltpu.VMEM((rows, c_pool - LANES), F32), pltpu.VMEM((rows, c_pool - 2 * LANES), F32)]
    return pl.pallas_call(
        _mix_prompt_kernel,
        grid=(b, nt),
        in_specs=in_specs,
        out_specs=out_specs,
        out_shape=out_shape,
        scratch_shapes=scratch,
        compiler_params=pltpu.CompilerParams(dimension_semantics=("arbitrary", "arbitrary"),
                                             vmem_limit_bytes=VMEM_LIMIT),
        name="mix_prompt",
    )(x, mod4, *weights)


def _mix_sample_kernel(x_ref, mod_ref, state_ref, ws0_ref, bs0_ref,
                       gmix_ref, gffn_ref, w_in_ref, pool_w_ref, pool_scale_ref,
                       ln_g_ref, ln_b_ref, w_out_ref, rw_hi_ref, rw_lo_ref, rbias_ref,
                       x1_ref, h2_ref, idx_ref, wt_ref, pool_ref, v_ref):
    x = x_ref[...]
    sh1, sc1, g1 = mod_ref[0], mod_ref[1], mod_ref[2]
    sh2, sc2 = mod_ref[3], mod_ref[4]
    a, u, v = _in_proj(x, gmix_ref[...], sc1, sh1, w_in_ref, ln_g_ref[...], ln_b_ref[...])
    v_ref[...] = v

    hist = [state_ref[:, POOL_BUF - j, :] for j in range(1, POOL_BUF + 1)]
    d_groups = []
    for gi, w in enumerate(POOL_WINDOWS):
        sl = slice(gi * LANES, (gi + 1) * LANES)
        s = a[:, sl]
        for j in range(1, w):
            s = s + hist[j - 1][:, sl]
        cnt = float(min(PAST_LEN + 1, w))
        d_groups.append(s / cnt - a[:, sl])
    pool_out = _pool_project(d_groups, pool_w_ref, pool_scale_ref[...])
    pool_ref[:, 0:POOL_BUF - 1, :] = state_ref[:, 1:POOL_BUF, :]
    pool_ref[:, POOL_BUF - 1, :] = a

    gate = u * (v * ws0_ref[...] + bs0_ref[...])

    x1, h2w, idx, wt = _post_mix(x, pool_out, gate, g1, w_out_ref, gffn_ref[...], sc2, sh2,
                                 rw_hi_ref, rw_lo_ref, rbias_ref[...])
    x1_ref[...] = x1
    _store_packed_rows(h2_ref, h2w)
    idx_ref[...] = idx
    wt_ref[...] = wt


def _mix_sample(xs, mod_s, state, ws0, bs0, weights):
    (gmix, gffn, w_in, pool_w, pool_scale, ln_g, ln_b, _ws, _bsb, w_out, rw_hi, rw_lo, rbias) = weights
    wlist = (gmix, gffn, w_in, pool_w, pool_scale, ln_g, ln_b, w_out, rw_hi, rw_lo, rbias)
    nb, d = xs.shape
    c_pool = pool_scale.shape[1]
    c_gate = d - c_pool
    nj = d // (2 * LANES)
    in_specs = [_full(xs.shape), _full(mod_s.shape), _full(state.shape), _full(ws0.shape), _full(bs0.shape)]
    in_specs += [_full(w.shape) for w in wlist]
    shapes = ((nb, d), (nb, nj, LANES), (TOP_K, nb), (TOP_K, nb), (nb, POOL_BUF, c_pool), (nb, c_gate))
    dtypes = (F32, U32, I32, F32, F32, F32)
    return pl.pallas_call(
        _mix_sample_kernel,
        grid=(1,),
        in_specs=in_specs,
        out_specs=tuple(_full(s) for s in shapes),
        out_shape=tuple(jax.ShapeDtypeStruct(s, t) for s, t in zip(shapes, dtypes)),
        compiler_params=pltpu.CompilerParams(dimension_semantics=("arbitrary",),
                                             vmem_limit_bytes=VMEM_LIMIT),
        name="mix_sample",
    )(xs, mod_s, state, ws0, bs0, *wlist)


def _route_kernel(idx_ref, pos_ref, blk_ref, nblk_ref, cnt_acc):
    n_e = cnt_acc.shape[0]
    n_tb = idx_ref.shape[0]
    tb = idx_ref.shape[2]
    eid = lax.broadcasted_iota(I32, (n_e, tb), 0)

    def multihot(i):
        blk = idx_ref[i]
        m = jnp.zeros((n_e, tb), F32)
        for k in range(TOP_K):
            m = m + jnp.where(eid == blk[k:k + 1], 1.0, 0.0)
        return blk, m

    cnt_acc[...] = jnp.zeros(cnt_acc.shape, F32)

    def count_body(i, c):
        _, m = multihot(i)
        cnt_acc[...] += m
        return c

    lax.fori_loop(0, n_tb, count_body, 0)
    counts = jnp.sum(cnt_acc[...], axis=1, keepdims=True).astype(I32)
    shift = ROW_BLOCK.bit_length() - 1
    padded = ((counts + (ROW_BLOCK - 1)) >> shift) << shift

    ri = lax.broadcasted_iota(I32, (n_e, n_e), 0)
    ci = lax.broadcasted_iota(I32, (n_e, n_e), 1)
    lower = jnp.where(ci < ri, 1.0, 0.0).astype(BF16)
    start = jnp.zeros((n_e, tb), F32)
    for dgt in range(3):
        digit = ((padded >> (7 * dgt)) & 127).astype(F32)
        digit = jnp.broadcast_to(digit, (n_e, tb)).astype(BF16)
        start = start + _dot(lower, digit) * float(128 ** dgt)
    pad_end = start.astype(I32) + padded

    nb_lanes = blk_ref.shape[1]
    for c in range(nb_lanes // tb):
        b0 = (lax.broadcasted_iota(I32, (n_e, tb), 1) + c * tb) * ROW_BLOCK
        e_of = jnp.sum(jnp.where(pad_end <= b0, 1.0, 0.0), axis=0, keepdims=True).astype(I32)
        blk_ref[:, c * tb:(c + 1) * tb] = jnp.minimum(e_of, n_e - 1)
    nblk_ref[...] = pad_end[n_e - 1:n_e, :] >> shift

    si = lax.broadcasted_iota(I32, (tb, tb), 0)
    ti = lax.broadcasted_iota(I32, (tb, tb), 1)
    upper = jnp.where(si < ti, 1.0, 0.0).astype(BF16)
    cnt_acc[...] = start

    def pos_body(i, c):
        blk, m = multihot(i)
        base = cnt_acc[...] + _dot(m.astype(BF16), upper)
        rows = []
        for k in range(TOP_K):
            rows.append(jnp.sum(jnp.where(eid == blk[k:k + 1], base, 0.0), axis=0, keepdims=True))
        pos_ref[i] = jnp.concatenate(rows, axis=0).astype(I32)
        cnt_acc[...] += jnp.broadcast_to(jnp.sum(m, axis=1, keepdims=True), m.shape)
        return c

    lax.fori_loop(0, n_tb, pos_body, 0)


def _route(idx3, n_e, n_blk_lanes):
    n_tb, k, tb = idx3.shape
    return pl.pallas_call(
        _route_kernel,
        in_specs=[_full(idx3.shape)],
        out_specs=(_full(idx3.shape), _full((1, n_blk_lanes)), _full((1, tb))),
        out_shape=(jax.ShapeDtypeStruct(idx3.shape, I32),
                   jax.ShapeDtypeStruct((1, n_blk_lanes), I32),
                   jax.ShapeDtypeStruct((1, tb), I32)),
        grid=(1,),
        scratch_shapes=[pltpu.VMEM((n_e, tb), F32)],
        compiler_params=pltpu.CompilerParams(dimension_semantics=("arbitrary",),
                                             vmem_limit_bytes=VMEM_LIMIT),
        name="route",
    )(idx3)


def _row_copy(src, dst, sem):
    return pltpu.make_async_copy(src, dst, sem)


def _dispatch_kernel(pos_ref, hp_ref, hs_ref, xs_in, xs_ref, sem, *, n_prompt_blocks):
    del xs_in
    i = pl.program_id(0)
    tb = hp_ref.shape[0]

    def scatter_rows(h_ref):
        def issue(j, c):
            for k in range(TOP_K):
                _row_copy(h_ref.at[j], xs_ref.at[pos_ref[0, k, j]], sem).start()
            return c

        lax.fori_loop(0, tb, issue, 0)

    @pl.when(i < n_prompt_blocks)
    def _():
        scatter_rows(hp_ref)

    @pl.when(i == n_prompt_blocks)
    def _():
        scatter_rows(hs_ref)

    def drain(j, c):
        for k in range(TOP_K):
            _row_copy(hp_ref.at[0], xs_ref.at[0], sem).wait()
        return c

    lax.fori_loop(0, tb, drain, 0)


def _dispatch(pos3, h2_p, h2_s, n_slots):
    n_tb, k, tb = pos3.shape
    nj = h2_p.shape[1]
    n_pb = h2_p.shape[0] // tb
    assert n_tb == n_pb + 1 and h2_s.shape[0] == tb
    xs0 = jnp.zeros((n_slots, nj, LANES), U32)
    return pl.pallas_call(
        functools.partial(_dispatch_kernel, n_prompt_blocks=n_pb),
        grid=(n_tb,),
        in_specs=[pl.BlockSpec((1, k, tb), lambda i: (i, 0, 0), memory_space=pltpu.SMEM),
                  pl.BlockSpec((tb, nj, LANES), lambda i: (jnp.minimum(i, n_pb - 1), 0, 0)),
                  _full(h2_s.shape),
                  pl.BlockSpec(memory_space=pl.ANY)],
        out_specs=pl.BlockSpec(memory_space=pl.ANY),
        out_shape=jax.ShapeDtypeStruct(xs0.shape, U32),
        input_output_aliases={3: 0},
        scratch_shapes=[pltpu.SemaphoreType.DMA],
        compiler_params=pltpu.CompilerParams(dimension_semantics=("arbitrary",),
                                             vmem_limit_bytes=VMEM_LIMIT),
        name="dispatch",
    )(pos3, h2_p, h2_s, xs0)


def _expert_kernel(blk_e_ref, nblk_ref, x_ref, wg_ref, wu_ref, wd_ref, o_ref, wg_s, wu_s, wd_s):
    i = pl.program_id(0)
    live = i < nblk_ref[0]
    e = blk_e_ref[i]
    e_prev = blk_e_ref[jnp.maximum(i - 1, 0)]

    @pl.when(jnp.logical_and(live, jnp.logical_or(i == 0, e != e_prev)))
    def _():
        wg_s[...] = wg_ref[0].astype(BF16)
        wu_s[...] = wu_ref[0].astype(BF16)
        wd_s[...] = wd_ref[0].astype(BF16)

    @pl.when(live)
    def _():
        x = _load_packed_rows(x_ref)
        hmid = _silu(_dot(x, wg_s[...])) * _dot(x, wu_s[...])
        out = _dot(hmid.astype(BF16), wd_s[...])
        _store_packed_rows(o_ref, _pack_rows(out))

    @pl.when(jnp.logical_not(live))
    def _():
        o_ref[...] = jnp.zeros(o_ref.shape, U32)


def _experts(blk_e, nblk, xs, we_gate, we_up, we_down):
    n_slots, nj, _ = xs.shape
    n_e, d, d_e = we_gate.shape
    tm = ROW_BLOCK
    n_blocks = n_slots // tm

    def row_map(i, be, nb):
        return (jnp.minimum(i, nb[0] - 1), 0, 0)

    def w_map(i, be, nb):
        return (be[jnp.minimum(i, nb[0] - 1)], 0, 0)

    grid_spec = pltpu.PrefetchScalarGridSpec(
        num_scalar_prefetch=2,
        grid=(n_blocks,),
        in_specs=[pl.BlockSpec((tm, nj, LANES), row_map),
                  pl.BlockSpec((1, d, d_e), w_map),
                  pl.BlockSpec((1, d, d_e), w_map),
                  pl.BlockSpec((1, d_e, d), w_map)],
        out_specs=pl.BlockSpec((tm, nj, LANES), lambda i, be, nb: (i, 0, 0)),
        scratch_shapes=[pltpu.VMEM((d, d_e), BF16), pltpu.VMEM((d, d_e), BF16),
                        pltpu.VMEM((d_e, d), BF16)])
    return pl.pallas_call(
        _expert_kernel,
        grid_spec=grid_spec,
        out_shape=jax.ShapeDtypeStruct(xs.shape, U32),
        compiler_params=pltpu.CompilerParams(dimension_semantics=("arbitrary",),
                                             vmem_limit_bytes=VMEM_LIMIT),
        name="experts",
    )(blk_e, nblk, xs, we_gate, we_up, we_down)


def _combine_kernel(pos_ref, wt_ref, hp_ref, hs_ref, x1p_ref, x1s_ref, g2p_ref, g2s_ref, gfin_ref,
                    wsg_ref, wsu_ref, wsd_ref, os_ref,
                    yp_ref, ys_ref, rows, sem, *, n_prompt_blocks):
    i = pl.program_id(0)
    tb = hp_ref.shape[0]
    is_prompt = i < n_prompt_blocks

    def issue(j, c):
        for k in range(TOP_K):
            _row_copy(os_ref.at[pos_ref[0, k, j]], rows.at[k, j], sem).start()
        return c

    lax.fori_loop(0, tb, issue, 0)

    hb = jnp.where(is_prompt, _load_packed_rows(hp_ref), _load_packed_rows(hs_ref))
    shared =_dot((_silu(_dot(hb, wsg_ref[...])) * _dot(hb, wsu_ref[...])).astype(BF16), wsd_ref[...])

    def drain(j, c):
        for k in range(TOP_K):
            _row_copy(os_ref.at[0], rows.at[0, 0], sem).wait()
        return c

    lax.fori_loop(0, tb, drain, 0)

    wcol = jnp.transpose(wt_ref[...])
    nj = hp_ref.shape[1]
    lo_acc = [None] * nj
    hi_acc = [None] * nj
    for k in range(TOP_K):
        wk = wcol[:, k:k + 1]
        for j in range(nj):
            lo, hi = _unpack_pair(rows[k, :, j, :])
            lo_acc[j] = lo * wk if k == 0 else lo_acc[j] + lo * wk
            hi_acc[j] = hi * wk if k == 0 else hi_acc[j] + hi * wk
    routed = jnp.concatenate(lo_acc + hi_acc, axis=-1)
    ffn = routed + shared

    def finish(x1, g2):
        return _rms(x1 + g2 * ffn, gfin_ref[...])

    blocks_per_seq = n_prompt_blocks // g2p_ref.shape[0]

    @pl.when(is_prompt)
    def _():
        yp_ref[...] = finish(x1p_ref[...], g2p_ref[i // blocks_per_seq])

    @pl.when(jnp.logical_not(is_prompt))
    def _():
        ys_ref[...] = finish(x1s_ref[...], g2s_ref[...])


def _combine(pos3, wt, h2_p, h2_s, x1_p, x1_s, g2p, g2s, gfin, wsg, wsu, wsd, o_sorted):
    n_tb, k, tb = pos3.shape
    n_prompt, d = x1_p.shape
    nj = h2_p.shape[1]
    n_pb = n_prompt // tb
    n_s = x1_s.shape[0]

    def clamp(i):
        return jnp.minimum(i, n_pb - 1)

    in_specs = [pl.BlockSpec((1, k, tb), lambda i: (i, 0, 0), memory_space=pltpu.SMEM),
                pl.BlockSpec((k, tb), lambda i: (0, i)),
                pl.BlockSpec((tb, nj, LANES), lambda i: (clamp(i), 0, 0)),
                _full(h2_s.shape),
                pl.BlockSpec((tb, d), lambda i: (clamp(i), 0)),
                _full(x1_s.shape),
                _full(g2p.shape), _full(g2s.shape), _full(gfin.shape),
                _full(wsg.shape), _full(wsu.shape), _full(wsd.shape),
                pl.BlockSpec(memory_space=pl.ANY)]
    out_specs = (pl.BlockSpec((tb, d), lambda i: (jnp.minimum(i, n_pb - 1), 0)),
                 pl.BlockSpec((n_s, d), lambda i: (0, 0)))
    assert n_tb == n_pb + 1 and n_s == tb and n_pb % g2p.shape[0] == 0
    return pl.pallas_call(
        functools.partial(_combine_kernel, n_prompt_blocks=n_pb),
        grid=(n_tb,),
        in_specs=in_specs,
        out_specs=out_specs,
        out_shape=(jax.ShapeDtypeStruct((n_prompt, d), F32), jax.ShapeDtypeStruct((n_s, d), F32)),
        scratch_shapes=[pltpu.VMEM((k, tb, nj, LANES), U32), pltpu.SemaphoreType.DMA],
        compiler_params=pltpu.CompilerParams(dimension_semantics=("arbitrary",),
                                             vmem_limit_bytes=VMEM_LIMIT),
        name="combine",
    )(pos3, wt, h2_p, h2_s, x1_p, x1_s, g2p, g2s, gfin, wsg, wsu, wsd, o_sorted)


def kernel(x_prompt, x_sample, state_pool, c_prompt, c_sample, norm_mix_g, norm_ffn_g, norm_final_g,
           w_ada, b_ada, w_in, pool_w, pool_scale, gate_ln_g, gate_ln_b, w_s, b_s, w_out,
           router_w, router_bias, we_gate, we_up, we_down, ws_gate, ws_up, ws_down):
    depth = w_ada.shape[0]
    assert depth == 1, "single-layer trunk"
    b, t, d = x_prompt.shape
    nb_s, t_s, _ = x_sample.shape
    assert t_s == 1 and nb_s == TOKEN_BLOCK and t % MIX_TILE == 0 and t % CHUNK == 0
    n_prompt = b * t
    n_tok = n_prompt + nb_s
    n_e = router_w.shape[2]
    c_pool = pool_scale.shape[1]
    c_gate = d - c_pool
    n_heads = w_s.shape[1]
    assert c_gate == n_heads * LANES and c_pool == len(POOL_WINDOWS) * LANES and w_s.shape[2] == CHUNK

    tril = jnp.tril(jnp.ones((CHUNK, CHUNK), dtype=bool))
    ws_m = jnp.where(tril[None], w_s[0], 0.0).astype(BF16)
    bsb = jnp.broadcast_to(b_s[0][:, :, None], (n_heads, CHUNK, LANES)).astype(F32)
    rw_t = router_w[0].T
    rw_hi = rw_t.astype(BF16)
    rw_lo = (rw_t - rw_hi.astype(F32)).astype(BF16)
    rbias = jnp.broadcast_to(router_bias[0][:, None], (n_e, LANES)).astype(F32)
    weights = (norm_mix_g[0][None], norm_ffn_g[0][None], w_in[0].astype(BF16), pool_w[0].astype(BF16),
               pool_scale[0][None], gate_ln_g[0].reshape(1, c_gate), gate_ln_b[0].reshape(1, c_gate),
               ws_m, bsb, w_out[0].astype(BF16), rw_hi, rw_lo, rbias)
    ws0 = jnp.repeat(w_s[0, :, 0, 0], LANES)[None]
    bs0 = jnp.repeat(b_s[0, :, 0], LANES)[None]

    mod = _modulation(jnp.concatenate([c_prompt, c_sample], axis=0), w_ada[0], b_ada[0])
    mod4 = mod.reshape(6, b + nb_s, 1, d)
    mod_s = mod[:, b:, :]

    x1_p, h2_p, idx_p, wt_p, pool_p = _mix_prompt(x_prompt, mod4, weights)
    x1_s, h2_s, idx_s, wt_s, pool_s, v_s = _mix_sample(
        x_sample.reshape(nb_s, d), mod_s, state_pool[0], ws0, bs0, weights)

    n_tb = n_tok // TOKEN_BLOCK
    idx = jnp.concatenate([idx_p, idx_s], axis=1)
    wt = jnp.concatenate([wt_p, wt_s], axis=1)
    idx3 = idx.reshape(TOP_K, n_tb, TOKEN_BLOCK).transpose(1, 0, 2)
    n_blocks = (n_tok * TOP_K) // ROW_BLOCK + n_e
    n_blk_lanes = -(-n_blocks // LANES) * LANES
    pos3, blk_e, nblk = _route(idx3, n_e, n_blk_lanes)

    n_slots = n_blocks * ROW_BLOCK
    xs = _dispatch(pos3, h2_p, h2_s, n_slots)
    o_sorted = _experts(blk_e[0], nblk[0, :1], xs, we_gate[0], we_up[0], we_down[0])

    g2p = mod[5, :b][:, None, :]
    g2s = mod[5, b:]
    y_p, y_s = _combine(pos3, wt, h2_p, h2_s, x1_p, x1_s, g2p, g2s, norm_final_g[None],
                        ws_gate[0].astype(BF16), ws_up[0].astype(BF16), ws_down[0].astype(BF16),
                        o_sorted)

    return (y_p.reshape(b, t, d), y_s.reshape(nb_s, 1, d),
            pool_p[None], pool_s[None], v_s.reshape(1, nb_s, 1, n_heads, LANES))
```
